```python
import jax, jax.numpy as jnp
from jax import lax
import numpy as np

D_MODEL = 2048
BATCH = 2
SEQ = 8192
DEPTH = 4

CHUNK = 64
N_MIXERS = 2
N_HEADS = 16
HEAD_DIM = D_MODEL // N_HEADS
Q_BLOCK = 128
SGU_CHUNK = 128
SGU_WIDTH = D_MODEL
SGU_GROUPS = 16
SGU_GROUP_DIM = SGU_WIDTH // SGU_GROUPS
D_FF = 4 * D_MODEL
N_FOX = (DEPTH + 1) // 2
N_SGU = DEPTH // 2
EPS = 1e-6
NEG_INF = -1e30

kernel_name = "fox_gmlp_interleaved_hybrid"


def rmsnorm(x, g):
    xf = x.astype(jnp.float32)
    y = xf * lax.rsqrt(jnp.mean(xf * xf, axis=-1, keepdims=True) + EPS)
    return (y * g.astype(jnp.float32)).astype(x.dtype)


def forgetting_attention(h, w_in, b_f, g_q, g_k, w_out):
    B, S, _ = h.shape
    proj = h @ w_in
    q, k, v, f_logit = jnp.split(proj, [D_MODEL, 2 * D_MODEL, 3 * D_MODEL], axis=-1)
    q = q.reshape(B, S, N_HEADS, HEAD_DIM)
    k = k.reshape(B, S, N_HEADS, HEAD_DIM)
    v = v.reshape(B, S, N_HEADS, HEAD_DIM)
    q = rmsnorm(q, g_q).astype(jnp.float32) * (HEAD_DIM ** -0.5)
    k = rmsnorm(k, g_k).astype(jnp.float32)
    log_f = jax.nn.log_sigmoid((f_logit + b_f).astype(jnp.float32))
    c = jnp.cumsum(log_f, axis=1).transpose(0, 2, 1)
    outs = []
    for i in range(S // Q_BLOCK):
        q0 = i * Q_BLOCK
        kend = q0 + Q_BLOCK
        qb = q[:, q0:kend]
        kb = k[:, :kend]
        vb = v[:, :kend]
        s = jnp.einsum('bqhd,bkhd->bhqk', qb, kb)
        s = s + (c[:, :, q0:kend, None] - c[:, :, None, :kend])
        q_pos = q0 + jnp.arange(Q_BLOCK)
        k_pos = jnp.arange(kend)
        mask = k_pos[None, :] <= q_pos[:, None]
        s = jnp.where(mask[None, None], s, NEG_INF)
        p = jax.nn.softmax(s, axis=-1)
        outs.append(jnp.einsum('bhqk,bkhd->bqhd', p.astype(vb.dtype), vb))
    o = jnp.concatenate(outs, axis=1).reshape(B, S, D_MODEL)
    return o @ w_out


def spatial_gating(h, w_in, b_in, g_v, w_s, b_s, w_out):
    B, S, _ = h.shape
    uv = jax.nn.gelu(h @ w_in + b_in)
    u, v = jnp.split(uv, 2, axis=-1)
    n_chunks = S // SGU_CHUNK
    v = v.reshape(B, n_chunks, SGU_CHUNK, SGU_GROUPS, SGU_GROUP_DIM)
    v = rmsnorm(v, g_v)
    tril = jnp.tril(jnp.ones((SGU_CHUNK, SGU_CHUNK), dtype=bool))
    w_causal = jnp.where(tril[None], w_s, jnp.zeros_like(w_s))
    mixed = jnp.einsum('gts,bnsgd->bntgd', w_causal, v)
    mixed = mixed + b_s.T[None, None, :, :, None]
    gated = u * mixed.reshape(B, S, SGU_WIDTH)
    return gated @ w_out


def squared_relu_mlp(h, w1, w2):
    a = jax.nn.relu(h @ w1)
    return (a * a) @ w2


def setup_inputs(seed: int = 0) -> dict:
    key = jax.random.key(seed)
    ks = jax.random.split(key, 20)
    f32 = jnp.float32
    d_in = D_MODEL ** -0.5
    x = jax.random.normal(ks[0], (BATCH, SEQ, D_MODEL), f32)
    g_mix = 1.0 + 0.05 * jax.random.normal(ks[1], (DEPTH, D_MODEL), f32)
    g_mlp = 1.0 + 0.05 * jax.random.normal(ks[2], (DEPTH, D_MODEL), f32)
    fox_w_in = d_in * jax.random.normal(ks[3], (N_FOX, D_MODEL, 3 * D_MODEL + N_HEADS), f32)
    fox_b_f = jnp.linspace(2.0, 6.0, N_HEADS, dtype=f32)[None, :] + 0.1 * jax.random.normal(ks[4], (N_FOX, N_HEADS), f32)
    fox_g_q = 1.0 + 0.05 * jax.random.normal(ks[5], (N_FOX, HEAD_DIM), f32)
    fox_g_k = 1.0 + 0.05 * jax.random.normal(ks[6], (N_FOX, HEAD_DIM), f32)
    fox_w_out = d_in * jax.random.normal(ks[7], (N_FOX, D_MODEL, D_MODEL), f32)
    sgu_w_in = d_in * jax.random.normal(ks[8], (N_SGU, D_MODEL, 2 * SGU_WIDTH), f32)
    sgu_b_in = 0.02 * jax.random.normal(ks[9], (N_SGU, 2 * SGU_WIDTH), f32)
    sgu_g_v = 1.0 + 0.05 * jax.random.normal(ks[10], (N_SGU, SGU_GROUPS, SGU_GROUP_DIM), f32)
    sgu_w_s = 0.5 * SGU_CHUNK ** -0.5 * jax.random.normal(ks[11], (N_SGU, SGU_GROUPS, SGU_CHUNK, SGU_CHUNK), f32)
    sgu_b_s = 1.0 + 0.1 * jax.random.normal(ks[12], (N_SGU, SGU_GROUPS, SGU_CHUNK), f32)
    sgu_w_out = SGU_WIDTH ** -0.5 * jax.random.normal(ks[13], (N_SGU, SGU_WIDTH, D_MODEL), f32)
    mlp_w1 = d_in * jax.random.normal(ks[14], (DEPTH, D_MODEL, D_FF), f32)
    mlp_w2 = 0.5 * D_FF ** -0.5 * jax.random.normal(ks[15], (DEPTH, D_FF, D_MODEL), f32)
    return {"x": x, "g_mix": g_mix, "g_mlp": g_mlp,
            "fox_w_in": fox_w_in, "fox_b_f": fox_b_f, "fox_g_q": fox_g_q, "fox_g_k": fox_g_k, "fox_w_out": fox_w_out,
            "sgu_w_in": sgu_w_in, "sgu_b_in": sgu_b_in, "sgu_g_v": sgu_g_v, "sgu_w_s": sgu_w_s, "sgu_b_s": sgu_b_s, "sgu_w_out": sgu_w_out,
            "mlp_w1": mlp_w1, "mlp_w2": mlp_w2}


def reference(x, g_mix, g_mlp, fox_w_in, fox_b_f, fox_g_q, fox_g_k, fox_w_out,
              sgu_w_in, sgu_b_in, sgu_g_v, sgu_w_s, sgu_b_s, sgu_w_out, mlp_w1, mlp_w2):
    for i in range(DEPTH):
        h = rmsnorm(x, g_mix[i])
        j = i // N_MIXERS
        if i % N_MIXERS == 0:
            x = x + forgetting_attention(h, fox_w_in[j], fox_b_f[j], fox_g_q[j], fox_g_k[j], fox_w_out[j])
        else:
            x = x + spatial_gating(h, sgu_w_in[j], sgu_b_in[j], sgu_g_v[j], sgu_w_s[j], sgu_b_s[j], sgu_w_out[j])
        h = rmsnorm(x, g_mlp[i])
        x = x + squared_relu_mlp(h, mlp_w1[i], mlp_w2[i])
    return x
```

```python
import functools
import math

import jax
import jax.numpy as jnp
from jax import lax
from jax.experimental import pallas as pl
from jax.experimental.pallas import tpu as pltpu

F32 = jnp.float32
BF16 = jnp.bfloat16

EPS = 1e-6
NEG_INF = -1e30
LANES = 128
N_HEADS = 16
HEAD_DIM = 128
SGU_CHUNK = 128
SGU_GROUPS = 16
VMEM_LIMIT_BYTES = 56 * 1024 * 1024


def _params(*sem):
    return pltpu.CompilerParams(dimension_semantics=sem,
                                vmem_limit_bytes=VMEM_LIMIT_BYTES)


def _rmsnorm_rows(x, g):
    ms = jnp.mean(x * x, axis=-1, keepdims=True)
    return x * lax.rsqrt(ms + EPS) * g


def _gelu_tanh(x):
    c = math.sqrt(2.0 / math.pi)
    return 0.5 * x * (1.0 + jnp.tanh(c * (x + 0.044715 * (x * x * x))))


def _relu2(x):
    a = jnp.maximum(x, 0.0)
    return a * a


_ACTS = {"none": lambda x: x, "relu2": _relu2, "gelu": _gelu_tanh}


def _norm_matmul_body(x_ref, g_ref, w_ref, b_ref, gn_ref, o_ref, xn_ref, *,
                      act, norm_lo, norm_hi):
    j = pl.program_id(1)

    @pl.when(j == 0)
    def _():
        xn_ref[...] = _rmsnorm_rows(x_ref[...], g_ref[...]).astype(BF16)

    acc = jnp.dot(xn_ref[...], w_ref[...], preferred_element_type=F32)
    a = _ACTS[act](acc + b_ref[...])
    bn = o_ref.shape[1]

    if norm_hi <= norm_lo:
        o_ref[...] = a.astype(o_ref.dtype)
        return

    in_norm = jnp.logical_and(j >= norm_lo, j < norm_hi)

    @pl.when(in_norm)
    def _():
        for c in range(bn // LANES):
            sl = slice(c * LANES, (c + 1) * LANES)
            o_ref[:, sl] = _rmsnorm_rows(a[:, sl], gn_ref[:, sl]).astype(o_ref.dtype)

    @pl.when(jnp.logical_not(in_norm))
    def _():
        o_ref[...] = a.astype(o_ref.dtype)


def _norm_matmul(x, g, w, bias, gain, *, act, norm_cols, bm, bn):
    m, k = x.shape
    n = w.shape[1]
    lo, hi = norm_cols
    body = functools.partial(_norm_matmul_body, act=act,
                             norm_lo=lo // bn, norm_hi=hi // bn)
    return pl.pallas_call(
        body,
        grid=(m // bm, n // bn),
        in_specs=[
            pl.BlockSpec((bm, k), lambda i, j: (i, 0)),
            pl.BlockSpec((1, k), lambda i, j: (0, 0)),
            pl.BlockSpec((k, bn), lambda i, j: (0, j)),
            pl.BlockSpec((1, bn), lambda i, j: (0, j)),
            pl.BlockSpec((1, bn), lambda i, j: (0, j)),
        ],
        out_specs=pl.BlockSpec((bm, bn), lambda i, j: (i, j)),
        out_shape=jax.ShapeDtypeStruct((m, n), BF16),
        scratch_shapes=[pltpu.VMEM((bm, k), BF16)],
        compiler_params=_params("parallel", "arbitrary"),
        name="norm_matmul_" + act,
    )(x, g, w, bias, gain)


def _matmul_res_body(a_ref, w_ref, r_ref, o_ref):
    o_ref[...] = r_ref[...] + jnp.dot(a_ref[...], w_ref[...],
                                      preferred_element_type=F32)


def _matmul_res(a, w, res, *, bm, bn):
    m, k = a.shape
    n = w.shape[1]
    return pl.pallas_call(
        _matmul_res_body,
        grid=(m // bm, n // bn),
        in_specs=[
            pl.BlockSpec((bm, k), lambda i, j: (i, 0)),
            pl.BlockSpec((k, bn), lambda i, j: (0, j)),
            pl.BlockSpec((bm, bn), lambda i, j: (i, j)),
        ],
        out_specs=pl.BlockSpec((bm, bn), lambda i, j: (i, j)),
        out_shape=jax.ShapeDtypeStruct((m, n), F32),
        compiler_params=_params("parallel", "arbitrary"),
        name="matmul_res_k%d" % k,
    )(a, w, res)


def _split3(x):
    hi = x.astype(BF16).astype(F32)
    r = x - hi
    mid = r.astype(BF16).astype(F32)
    lo = (r - mid).astype(BF16).astype(F32)
    return hi, mid, lo


def _fox_gate_body(x_ref, g_ref, wf_ref, bf_ref, qt_ref, kt_ref, carry_ref):
    t = pl.program_id(1)
    bm = x_ref.shape[0]

    @pl.when(t == 0)
    def _():
        carry_ref[...] = jnp.zeros_like(carry_ref)

    xn = _rmsnorm_rows(x_ref[...], g_ref[...]).astype(BF16)
    fl = jnp.dot(xn, wf_ref[...], preferred_element_type=F32) + bf_ref[...]
    logf = jnp.minimum(fl, 0.0) - jnp.log1p(jnp.exp(-jnp.abs(fl)))

    row = lax.broadcasted_iota(jnp.int32, (bm, bm), 0)
    col = lax.broadcasted_iota(jnp.int32, (bm, bm), 1)
    tri = jnp.where(col <= row, 1.0, 0.0).astype(BF16)
    hi, mid, lo = _split3(logf)
    parts = jnp.concatenate([hi, mid, lo], axis=1).astype(BF16)
    ps = jnp.dot(tri, parts, preferred_element_type=F32)
    c = (ps[:, :LANES] + ps[:, LANES:2 * LANES] + ps[:, 2 * LANES:]) + carry_ref[...]
    carry_ref[...] = c[bm - 1:bm, :]

    lane = lax.broadcasted_iota(jnp.int32, (bm, LANES), 1)
    for h in range(N_HEADS):
        ch, cm, cl = _split3(jnp.broadcast_to(c[:, h:h + 1], (bm, LANES)))
        qt = jnp.where(lane == 0, ch, jnp.where(lane == 1, cm, jnp.where(
            lane == 2, cl, jnp.where(lane < 6, 1.0, 0.0))))
        kt = jnp.where(lane < 3, 1.0, jnp.where(lane == 3, -ch, jnp.where(
            lane == 4, -cm, jnp.where(lane == 5, -cl, 0.0))))
        qt_ref[0, h] = qt.astype(BF16)
        kt_ref[0, h] = kt.astype(BF16)


def _fox_gate(x, g, wf, bf, *, batch, seq, bm):
    m, k = x.shape
    nt = seq // bm
    tail = jax.ShapeDtypeStruct((batch, N_HEADS, seq, LANES), BF16)
    return pl.pallas_call(
        _fox_gate_body,
        grid=(batch, nt),
        in_specs=[
            pl.BlockSpec((bm, k), lambda b, t: (b * nt + t, 0)),
            pl.BlockSpec((1, k), lambda b, t: (0, 0)),
            pl.BlockSpec((k, LANES), lambda b, t: (0, 0)),
            pl.BlockSpec((1, LANES), lambda b, t: (0, 0)),
        ],
        out_specs=[
            pl.BlockSpec((1, N_HEADS, bm, LANES), lambda b, t: (b, 0, t, 0)),
            pl.BlockSpec((1, N_HEADS, bm, LANES), lambda b, t: (b, 0, t, 0)),
        ],
        out_shape=[tail, tail],
        scratch_shapes=[pltpu.VMEM((1, LANES), F32)],
        compiler_params=_params("parallel", "arbitrary"),
        name="fox_gate",
    )(x, g, wf, bf)


def _fox_attn_body(q_ref, qt_ref, k_ref, kt_ref, v_ref, o_ref, *, tq):
    qi = pl.program_id(2)
    q = jnp.concatenate([q_ref[...], qt_ref[0, 0]], axis=1)

    def step(j, carry, masked):
        m, l, acc = carry
        off = pl.multiple_of(j * tq, tq)
        k = jnp.concatenate([k_ref[pl.ds(off, tq), :],
                             kt_ref[0, 0, pl.ds(off, tq), :]], axis=1)
        s = lax.dot_general(q, k, (((1,), (1,)), ((), ())),
                            preferred_element_type=F32)
        if masked:
            row = lax.broadcasted_iota(jnp.int32, (tq, tq), 0)
            col = lax.broadcasted_iota(jnp.int32, (tq, tq), 1)
            s = jnp.where(col <= row, s, NEG_INF)
        m_new = jnp.maximum(m, jnp.max(s, axis=-1, keepdims=True))
        p = jnp.exp(s - m_new)
        alpha = jnp.exp(m - m_new)
        l = alpha * l + jnp.sum(p, axis=-1, keepdims=True)
        acc = alpha * acc + jnp.dot(p.astype(BF16), v_ref[pl.ds(off, tq), :],
                                    preferred_element_type=F32)
        return m_new, l, acc

    init = (jnp.full((tq, 1), NEG_INF, F32), jnp.zeros((tq, 1), F32),
            jnp.zeros((tq, HEAD_DIM), F32))
    carry = lax.fori_loop(0, qi, lambda j, c: step(j, c, False), init)
    _, l, acc = step(qi, carry, True)
    o_ref[...] = (acc / l).astype(o_ref.dtype)


def _fox_attention(qkv, qt, kt, *, batch, seq, tq):
    m = qkv.shape[0]
    nq = seq // tq
    d = N_HEADS * HEAD_DIM
    return pl.pallas_call(
        functools.partial(_fox_attn_body, tq=tq),
        grid=(batch, N_HEADS, nq),
        in_specs=[
            pl.BlockSpec((tq, HEAD_DIM), lambda b, h, i: (b * nq + i, h)),
            pl.BlockSpec((1, 1, tq, LANES), lambda b, h, i: (b, h, i, 0)),
            pl.BlockSpec((seq, HEAD_DIM), lambda b, h, i: (b, N_HEADS + h)),
            pl.BlockSpec((1, 1, seq, LANES), lambda b, h, i: (b, h, 0, 0)),
            pl.BlockSpec((seq, HEAD_DIM), lambda b, h, i: (b, 2 * N_HEADS + h)),
        ],
        out_specs=pl.BlockSpec((tq, HEAD_DIM), lambda b, h, i: (b * nq + i, h)),
        out_shape=jax.ShapeDtypeStruct((m, d), BF16),
        compiler_params=_params("parallel", "parallel", "arbitrary"),
        name="fox_attention",
    )(qkv, qt, qkv, kt, qkv)


def _sgu_out_body(u_ref, v_ref, ws_ref, bs_ref, w_ref, r_ref, o_ref, gated_ref):
    j = pl.program_id(1)
    bm = u_ref.shape[0]

    @pl.when(j == 0)
    def _():
        row = lax.broadcasted_iota(jnp.int32, (SGU_CHUNK, SGU_CHUNK), 0)
        col = lax.broadcasted_iota(jnp.int32, (SGU_CHUNK, SGU_CHUNK), 1)
        causal = col <= row
        for g in range(SGU_GROUPS):
            cs = slice(g * LANES, (g + 1) * LANES)
            wc = jnp.where(causal, ws_ref[g], jnp.zeros_like(ws_ref[g]))
            for c in range(bm // SGU_CHUNK):
                rs = slice(c * SGU_CHUNK, (c + 1) * SGU_CHUNK)
                mixed = jnp.dot(wc, v_ref[rs, cs],
                                preferred_element_type=F32) + bs_ref[g]
                gated_ref[rs, cs] = (u_ref[rs, cs].astype(F32) * mixed).astype(BF16)

    o_ref[...] = r_ref[...] + jnp.dot(gated_ref[...], w_ref[...],
                                      preferred_element_type=F32)


def _sgu_out(uv, ws, bsb, w, res, *, bm, bn):
    m = uv.shape[0]
    k, n = w.shape
    return pl.pallas_call(
        _sgu_out_body,
        grid=(m // bm, n // bn),
        in_specs=[
            pl.BlockSpec((bm, k), lambda i, j: (i, 0)),
            pl.BlockSpec((bm, k), lambda i, j: (i, 1)),
            pl.BlockSpec(ws.shape, lambda i, j: (0, 0, 0)),
            pl.BlockSpec(bsb.shape, lambda i, j: (0, 0, 0)),
            pl.BlockSpec((k, bn), lambda i, j: (0, j)),
            pl.BlockSpec((bm, bn), lambda i, j: (i, j)),
        ],
        out_specs=pl.BlockSpec((bm, bn), lambda i, j: (i, j)),
        out_shape=jax.ShapeDtypeStruct((m, n), F32),
        scratch_shapes=[pltpu.VMEM((bm, k), BF16)],
        compiler_params=_params("parallel", "arbitrary"),
        name="sgu_out",
    )(uv, uv, ws, bsb, w, res)


def _row(v):
    return v.reshape(1, -1).astype(F32)


def _mlp(x, g, w1, w2):
    n = w1.shape[1]
    zeros = jnp.zeros((1, n), F32)
    h = _norm_matmul(x, _row(g), w1.astype(BF16), zeros, zeros, act="relu2",
                     norm_cols=(0, 0), bm=1024, bn=1024)
    return _matmul_res(h, w2.astype(BF16), x, bm=512, bn=512)


def _fox_layer(x, g, w_in, b_f, g_q, g_k, w_out, *, batch, seq):
    d = N_HEADS * HEAD_DIM
    w_qkv = w_in[:, :3 * d].astype(BF16)
    w_f = jnp.pad(w_in[:, 3 * d:], ((0, 0), (0, LANES - N_HEADS))).astype(BF16)
    b_fp = jnp.pad(b_f, (0, LANES - N_HEADS)).reshape(1, LANES).astype(F32)
    gain = jnp.concatenate([jnp.tile(g_q * HEAD_DIM ** -0.5, N_HEADS),
                            jnp.tile(g_k, N_HEADS), jnp.ones((d,), F32)])
    qkv = _norm_matmul(x, _row(g), w_qkv, jnp.zeros((1, 3 * d), F32), _row(gain),
                       act="none", norm_cols=(0, 2 * d), bm=1024, bn=1024)
    qt, kt = _fox_gate(x, _row(g), w_f, b_fp, batch=batch, seq=seq, bm=512)
    o = _fox_attention(qkv, qt, kt, batch=batch, seq=seq, tq=512)
    return _matmul_res(o, w_out.astype(BF16), x, bm=1024, bn=1024)


def _sgu_layer(x, g, w_in, b_in, g_v, w_s, b_s, w_out):
    width = w_out.shape[0]
    gain = jnp.concatenate([jnp.ones((width,), F32), g_v.reshape(-1)])
    uv = _norm_matmul(x, _row(g), w_in.astype(BF16), _row(b_in), _row(gain),
                      act="gelu", norm_cols=(width, 2 * width), bm=1024, bn=1024)
    bsb = jnp.broadcast_to(b_s[:, :, None], b_s.shape + (LANES,)).astype(F32)
    return _sgu_out(uv, w_s.astype(BF16), bsb, w_out.astype(BF16), x, bm=1024, bn=1024)


def kernel(x, g_mix, g_mlp, fox_w_in, fox_b_f, fox_g_q, fox_g_k, fox_w_out,
           sgu_w_in, sgu_b_in, sgu_g_v, sgu_w_s, sgu_b_s, sgu_w_out, mlp_w1, mlp_w2):
    batch, seq, d = x.shape
    depth = g_mix.shape[0]
    x = x.reshape(batch * seq, d)
    for i in range(depth):
        j = i // 2
        if i % 2 == 0:
            x = _fox_layer(x, g_mix[i], fox_w_in[j], fox_b_f[j], fox_g_q[j],
                           fox_g_k[j], fox_w_out[j], batch=batch, seq=seq)
        else:
            x = _sgu_layer(x, g_mix[i], sgu_w_in[j], sgu_b_in[j], sgu_g_v[j],
                           sgu_w_s[j], sgu_b_s[j], sgu_w_out[j])
        x = _mlp(x, g_mlp[i], mlp_w1[i], mlp_w2[i])
    return x.reshape(batch, seq, d)
```

```python
import functools
import math

import jax
import jax.numpy as jnp
from jax import lax
from jax.experimental import pallas as pl
from jax.experimental.pallas import tpu as pltpu

F32 = jnp.float32
BF16 = jnp.bfloat16

EPS = 1e-6
NEG_INF = -1e30
LOG2E = math.log2(math.e)
LANES = 128
N_HEADS = 16
HEAD_DIM = 128
SGU_CHUNK = 128
SGU_GROUPS = 16
VMEM_LIMIT_BYTES = 56 * 1024 * 1024


def _params(*sem):
    return pltpu.CompilerParams(dimension_semantics=sem,
                                vmem_limit_bytes=VMEM_LIMIT_BYTES)


def _rmsnorm_rows(x, g):
    ms = jnp.mean(x * x, axis=-1, keepdims=True)
    return x * lax.rsqrt(ms + EPS) * g


def _gelu_tanh(x):
    c = math.sqrt(2.0 / math.pi)
    return 0.5 * x * (1.0 + jnp.tanh(c * (x + 0.044715 * (x * x * x))))


def _relu2(x):
    a = jnp.maximum(x, 0.0)
    return a * a


_ACTS = {"none": lambda x: x, "gelu": _gelu_tanh}


def _norm_matmul_body(x_ref, g_ref, w_ref, b_ref, gn_ref, o_ref, xn_ref, *,
                      act, norm_lo, norm_hi):
    j = pl.program_id(1)

    @pl.when(j == 0)
    def _():
        xn_ref[...] = _rmsnorm_rows(x_ref[...], g_ref[...]).astype(BF16)

    acc = jnp.dot(xn_ref[...], w_ref[...], preferred_element_type=F32)
    a = _ACTS[act](acc + b_ref[...])
    bn = o_ref.shape[1]

    if norm_hi <= norm_lo:
        o_ref[...] = a.astype(o_ref.dtype)
        return

    in_norm = jnp.logical_and(j >= norm_lo, j < norm_hi)

    @pl.when(in_norm)
    def _():
        for c in range(bn // LANES):
            sl = slice(c * LANES, (c + 1) * LANES)
            o_ref[:, sl] = _rmsnorm_rows(a[:, sl], gn_ref[:, sl]).astype(o_ref.dtype)

    @pl.when(jnp.logical_not(in_norm))
    def _():
        o_ref[...] = a.astype(o_ref.dtype)


def _norm_matmul(x, g, w, bias, gain, *, act, norm_cols, bm, bn):
    m, k = x.shape
    n = w.shape[1]
    lo, hi = norm_cols
    body = functools.partial(_norm_matmul_body, act=act,
                             norm_lo=lo // bn, norm_hi=hi // bn)
    return pl.pallas_call(
        body,
        grid=(m // bm, n // bn),
        in_specs=[
            pl.BlockSpec((bm, k), lambda i, j: (i, 0)),
            pl.BlockSpec((1, k), lambda i, j: (0, 0)),
            pl.BlockSpec((k, bn), lambda i, j: (0, j)),
            pl.BlockSpec((1, bn), lambda i, j: (0, j)),
            pl.BlockSpec((1, bn), lambda i, j: (0, j)),
        ],
        out_specs=pl.BlockSpec((bm, bn), lambda i, j: (i, j)),
        out_shape=jax.ShapeDtypeStruct((m, n), BF16),
        scratch_shapes=[pltpu.VMEM((bm, k), BF16)],
        compiler_params=_params("parallel", "arbitrary"),
        name="norm_matmul_" + act,
    )(x, g, w, bias, gain)


def _matmul_res_body(a_ref, w_ref, r_ref, o_ref):
    o_ref[...] = r_ref[...] + jnp.dot(a_ref[...], w_ref[...],
                                      preferred_element_type=F32)


def _matmul_res(a, w, res, *, bm, bn):
    m, k = a.shape
    n = w.shape[1]
    return pl.pallas_call(
        _matmul_res_body,
        grid=(m // bm, n // bn),
        in_specs=[
            pl.BlockSpec((bm, k), lambda i, j: (i, 0)),
            pl.BlockSpec((k, bn), lambda i, j: (0, j)),
            pl.BlockSpec((bm, bn), lambda i, j: (i, j)),
        ],
        out_specs=pl.BlockSpec((bm, bn), lambda i, j: (i, j)),
        out_shape=jax.ShapeDtypeStruct((m, n), F32),
        compiler_params=_params("parallel", "arbitrary"),
        name="matmul_res_k%d" % k,
    )(a, w, res)


def _split3(x):
    hi = x.astype(BF16).astype(F32)
    r = x - hi
    mid = r.astype(BF16).astype(F32)
    lo = (r - mid).astype(BF16).astype(F32)
    return hi, mid, lo


def _fox_gate_body(x_ref, g_ref, wf_ref, bf_ref, qt_ref, kt_ref, carry_ref):
    t = pl.program_id(1)
    bm = x_ref.shape[0]

    @pl.when(t == 0)
    def _():
        carry_ref[...] = jnp.zeros_like(carry_ref)

    xn = _rmsnorm_rows(x_ref[...], g_ref[...]).astype(BF16)
    fl = jnp.dot(xn, wf_ref[...], preferred_element_type=F32) + bf_ref[...]
    logf = jnp.minimum(fl, 0.0) - jnp.log1p(jnp.exp(-jnp.abs(fl)))

    row = lax.broadcasted_iota(jnp.int32, (bm, bm), 0)
    col = lax.broadcasted_iota(jnp.int32, (bm, bm), 1)
    tri = jnp.where(col <= row, 1.0, 0.0).astype(BF16)
    hi, mid, lo = _split3(logf)
    parts = jnp.concatenate([hi, mid, lo], axis=1).astype(BF16)
    ps = jnp.dot(tri, parts, preferred_element_type=F32)
    c = (ps[:, :LANES] + ps[:, LANES:2 * LANES] + ps[:, 2 * LANES:]) + carry_ref[...]
    carry_ref[...] = c[bm - 1:bm, :]

    c2 = c * LOG2E
    lane = lax.broadcasted_iota(jnp.int32, (bm, LANES), 1)
    for h in range(N_HEADS):
        ch, cm, cl = _split3(jnp.broadcast_to(c2[:, h:h + 1], (bm, LANES)))
        qt = jnp.where(lane == 0, ch, jnp.where(lane == 1, cm, jnp.where(
            lane == 2, cl, jnp.where(lane < 6, 1.0, 0.0))))
        kt = jnp.where(lane < 3, 1.0, jnp.where(lane == 3, -ch, jnp.where(
            lane == 4, -cm, jnp.where(lane == 5, -cl, 0.0))))
        qt_ref[0, h] = qt.astype(BF16)
        kt_ref[0, h] = kt.astype(BF16)


def _fox_gate(x, g, wf, bf, *, batch, seq, bm):
    m, k = x.shape
    nt = seq // bm
    tail = jax.ShapeDtypeStruct((batch, N_HEADS, seq, LANES), BF16)
    return pl.pallas_call(
        _fox_gate_body,
        grid=(batch, nt),
        in_specs=[
            pl.BlockSpec((bm, k), lambda b, t: (b * nt + t, 0)),
            pl.BlockSpec((1, k), lambda b, t: (0, 0)),
            pl.BlockSpec((k, LANES), lambda b, t: (0, 0)),
            pl.BlockSpec((1, LANES), lambda b, t: (0, 0)),
        ],
        out_specs=[
            pl.BlockSpec((1, N_HEADS, bm, LANES), lambda b, t: (b, 0, t, 0)),
            pl.BlockSpec((1, N_HEADS, bm, LANES), lambda b, t: (b, 0, t, 0)),
        ],
        out_shape=[tail, tail],
        scratch_shapes=[pltpu.VMEM((1, LANES), F32)],
        compiler_params=_params("parallel", "arbitrary"),
        name="fox_gate",
    )(x, g, wf, bf)


V_ROWS = HEAD_DIM + 16


def _fox_attn_body(q_ref, qt_ref, k_ref, kt_ref, v_ref, o_ref,
                   vT_ref, mask_ref, qT_ref, s0_ref, s1_ref, p0_ref, p1_ref, acc_ref,
                   *, tq, tk):
    qi = pl.program_id(2)
    n_kv = v_ref.shape[0] // tk
    last = 2 * qi + 1

    @pl.when(qi == 0)
    def _():
        ones_row = jnp.where(
            lax.broadcasted_iota(jnp.int32, (V_ROWS - HEAD_DIM, tk), 0) == 0, 1.0, 0.0)
        for c in range(n_kv):
            blk = v_ref[c * tk:(c + 1) * tk, :].astype(F32)
            vT_ref[c, 0:HEAD_DIM, :] = blk.T.astype(BF16)
            vT_ref[c, HEAD_DIM:V_ROWS, :] = ones_row.astype(BF16)
        r = lax.broadcasted_iota(jnp.int32, (tk, tq), 0)
        c = lax.broadcasted_iota(jnp.int32, (tk, tq), 1)
        mask_ref[0] = jnp.zeros((tk, tq), F32)
        mask_ref[1] = jnp.where(r <= c, 0.0, NEG_INF)
        mask_ref[2] = jnp.where(r + tk <= c, 0.0, NEG_INF)

    qT_ref[0:HEAD_DIM, :] = q_ref[...].astype(F32).T.astype(BF16)
    qT_ref[HEAD_DIM:2 * HEAD_DIM, :] = qt_ref[0, 0].astype(F32).T.astype(BF16)
    acc_ref[...] = jnp.zeros_like(acc_ref)
    p1_ref[...] = jnp.zeros_like(p1_ref)

    def stage_a(j, s_ref):
        off = pl.multiple_of(j * tk, tk)
        k = jnp.concatenate([k_ref[pl.ds(off, tk), :],
                             kt_ref[0, 0, pl.ds(off, tk), :]], axis=1)
        s = jnp.dot(k, qT_ref[...], preferred_element_type=F32)
        s = s + mask_ref[jnp.clip(j - 2 * qi + 1, 0, 2)]
        s_ref[...] = s
        return jnp.max(s, axis=0, keepdims=True)

    def stage_c(j, alpha, p_ref):
        pv = jnp.dot(vT_ref[j], p_ref[...], preferred_element_type=F32)
        acc_ref[...] = alpha * acc_ref[...] + pv

    def half(j, carry, s_cur, s_nxt, p_cur, p_prv):
        m_prev, alpha_prev, cmax = carry
        m_new = jnp.maximum(m_prev, cmax)
        alpha = jnp.exp2(m_prev - m_new)
        p = jnp.exp2(s_cur[...] - m_new)
        stage_c(jnp.maximum(j - 1, 0), alpha_prev, p_prv)
        p_cur[...] = p.astype(BF16)
        cmax_next = stage_a(jnp.minimum(j + 1, last), s_nxt)
        return m_new, alpha, cmax_next

    def pair(t, carry):
        carry = half(2 * t, carry, s0_ref, s1_ref, p0_ref, p1_ref)
        return half(2 * t + 1, carry, s1_ref, s0_ref, p1_ref, p0_ref)

    init = (jnp.full((1, tq), NEG_INF, F32), jnp.ones((1, tq), F32),
            stage_a(0, s0_ref))
    _, alpha, _ = lax.fori_loop(0, qi + 1, pair, init)
    stage_c(last, alpha, p1_ref)

    acc = acc_ref[...]
    o = acc[0:HEAD_DIM, :] / acc[HEAD_DIM:HEAD_DIM + 1, :]
    o_ref[...] = o.T.astype(o_ref.dtype)


def _fox_attention(qkv, qt, kt, *, batch, seq, tq):
    m = qkv.shape[0]
    nq = seq // tq
    tk = tq // 2
    d = N_HEADS * HEAD_DIM
    return pl.pallas_call(
        functools.partial(_fox_attn_body, tq=tq, tk=tk),
        grid=(batch, N_HEADS, nq),
        in_specs=[
            pl.BlockSpec((tq, HEAD_DIM), lambda b, h, i: (b * nq + i, h)),
            pl.BlockSpec((1, 1, tq, LANES), lambda b, h, i: (b, h, i, 0)),
            pl.BlockSpec((seq, HEAD_DIM), lambda b, h, i: (b, N_HEADS + h)),
            pl.BlockSpec((1, 1, seq, LANES), lambda b, h, i: (b, h, 0, 0)),
            pl.BlockSpec((seq, HEAD_DIM), lambda b, h, i: (b, 2 * N_HEADS + h)),
        ],
        out_specs=pl.BlockSpec((tq, HEAD_DIM), lambda b, h, i: (b * nq + i, h)),
        out_shape=jax.ShapeDtypeStruct((m, d), BF16),
        scratch_shapes=[
            pltpu.VMEM((seq // tk, V_ROWS, tk), BF16),
            pltpu.VMEM((3, tk, tq), F32),
            pltpu.VMEM((2 * HEAD_DIM, tq), BF16),
            pltpu.VMEM((tk, tq), F32), pltpu.VMEM((tk, tq), F32),
            pltpu.VMEM((tk, tq), BF16), pltpu.VMEM((tk, tq), BF16),
            pltpu.VMEM((V_ROWS, tq), F32),
        ],
        compiler_params=_params("parallel", "parallel", "arbitrary"),
        name="fox_attention",
    )(qkv, qt, qkv, kt, qkv)


def _sgu_out_body(u_ref, v_ref, ws_ref, bs_ref, w_ref, r_ref, o_ref, gated_ref):
    j = pl.program_id(1)
    bm = u_ref.shape[0]

    @pl.when(j == 0)
    def _():
        row = lax.broadcasted_iota(jnp.int32, (SGU_CHUNK, SGU_CHUNK), 0)
        col = lax.broadcasted_iota(jnp.int32, (SGU_CHUNK, SGU_CHUNK), 1)
        causal = col <= row
        for g in range(SGU_GROUPS):
            cs = slice(g * LANES, (g + 1) * LANES)
            wc = jnp.where(causal, ws_ref[g], jnp.zeros_like(ws_ref[g]))
            for c in range(bm // SGU_CHUNK):
                rs = slice(c * SGU_CHUNK, (c + 1) * SGU_CHUNK)
                mixed = jnp.dot(wc, v_ref[rs, cs],
                                preferred_element_type=F32) + bs_ref[g]
                gated_ref[rs, cs] = (u_ref[rs, cs].astype(F32) * mixed).astype(BF16)

    o_ref[...] = r_ref[...] + jnp.dot(gated_ref[...], w_ref[...],
                                      preferred_element_type=F32)


def _sgu_out(uv, ws, bsb, w, res, *, bm, bn):
    m = uv.shape[0]
    k, n = w.shape
    return pl.pallas_call(
        _sgu_out_body,
        grid=(m // bm, n // bn),
        in_specs=[
            pl.BlockSpec((bm, k), lambda i, j: (i, 0)),
            pl.BlockSpec((bm, k), lambda i, j: (i, 1)),
            pl.BlockSpec(ws.shape, lambda i, j: (0, 0, 0)),
            pl.BlockSpec(bsb.shape, lambda i, j: (0, 0, 0)),
            pl.BlockSpec((k, bn), lambda i, j: (0, j)),
            pl.BlockSpec((bm, bn), lambda i, j: (i, j)),
        ],
        out_specs=pl.BlockSpec((bm, bn), lambda i, j: (i, j)),
        out_shape=jax.ShapeDtypeStruct((m, n), F32),
        scratch_shapes=[pltpu.VMEM((bm, k), BF16)],
        compiler_params=_params("parallel", "arbitrary"),
        name="sgu_out",
    )(uv, uv, ws, bsb, w, res)


def _row(v):
    return v.reshape(1, -1).astype(F32)


def _mlp_body(x_ref, g_ref, w1_ref, w2_ref, o_ref, xn_ref):
    @pl.when(pl.program_id(1) == 0)
    def _():
        x = x_ref[...]
        xn_ref[...] = _rmsnorm_rows(x, g_ref[...]).astype(BF16)
        o_ref[...] = x

    h = jnp.dot(xn_ref[...], w1_ref[...], preferred_element_type=F32)
    o_ref[...] += jnp.dot(_relu2(h).astype(BF16), w2_ref[...],
                          preferred_element_type=F32)


def _mlp(x, g, w1, w2, *, bm=1024, bf=512):
    m, d = x.shape
    f = w1.shape[1]
    return pl.pallas_call(
        _mlp_body,
        grid=(m // bm, f // bf),
        in_specs=[
            pl.BlockSpec((bm, d), lambda i, c: (i, 0)),
            pl.BlockSpec((1, d), lambda i, c: (0, 0)),
            pl.BlockSpec((d, bf), lambda i, c: (0, c)),
            pl.BlockSpec((bf, d), lambda i, c: (c, 0)),
        ],
        out_specs=pl.BlockSpec((bm, d), lambda i, c: (i, 0)),
        out_shape=jax.ShapeDtypeStruct((m, d), F32),
        scratch_shapes=[pltpu.VMEM((bm, d), BF16)],
        compiler_params=_params("parallel", "arbitrary"),
        name="mlp_relu2",
    )(x, _row(g), w1.astype(BF16), w2.astype(BF16))


def _fox_layer(x, g, w_in, b_f, g_q, g_k, w_out, *, batch, seq):
    d = N_HEADS * HEAD_DIM
    w_qkv = w_in[:, :3 * d].astype(BF16)
    w_f = jnp.pad(w_in[:, 3 * d:], ((0, 0), (0, LANES - N_HEADS))).astype(BF16)
    b_fp = jnp.pad(b_f, (0, LANES - N_HEADS)).reshape(1, LANES).astype(F32)
    gain = jnp.concatenate([jnp.tile(g_q * (HEAD_DIM ** -0.5 * LOG2E), N_HEADS),
                            jnp.tile(g_k, N_HEADS), jnp.ones((d,), F32)])
    qkv = _norm_matmul(x, _row(g), w_qkv, jnp.zeros((1, 3 * d), F32), _row(gain),
                       act="none", norm_cols=(0, 2 * d), bm=1024, bn=1024)
    qt, kt = _fox_gate(x, _row(g), w_f, b_fp, batch=batch, seq=seq, bm=512)
    o = _fox_attention(qkv, qt, kt, batch=batch, seq=seq, tq=1024)
    return _matmul_res(o, w_out.astype(BF16), x, bm=1024, bn=1024)


def _sgu_layer(x, g, w_in, b_in, g_v, w_s, b_s, w_out):
    width = w_out.shape[0]
    gain = jnp.concatenate([jnp.ones((width,), F32), g_v.reshape(-1)])
    uv = _norm_matmul(x, _row(g), w_in.astype(BF16), _row(b_in), _row(gain),
                      act="gelu", norm_cols=(width, 2 * width), bm=1024, bn=1024)
    bsb = jnp.broadcast_to(b_s[:, :, None], b_s.shape + (LANES,)).astype(F32)
    return _sgu_out(uv, w_s.astype(BF16), bsb, w_out.astype(BF16), x, bm=1024, bn=1024)


def kernel(x, g_mix, g_mlp, fox_w_in, fox_b_f, fox_g_q, fox_g_k, fox_w_out,
           sgu_w_in, sgu_b_in, sgu_g_v, sgu_w_s, sgu_b_s, sgu_w_out, mlp_w1, mlp_w2):
    batch, seq, d = x.shape
    depth = g_mix.shape[0]
    x = x.reshape(batch * seq, d)
    for i in range(depth):
        j = i // 2
        if i % 2 == 0:
            x = _fox_layer(x, g_mix[i], fox_w_in[j], fox_b_f[j], fox_g_q[j],
                           fox_g_k[j], fox_w_out[j], batch=batch, seq=seq)
        else:
            x = _sgu_layer(x, g_mix[i], sgu_w_in[j], sgu_b_in[j], sgu_g_v[j],
                           sgu_w_s[j], sgu_b_s[j], sgu_w_out[j])
        x = _mlp(x, g_mlp[i], mlp_w1[i], mlp_w2[i])
    return x.reshape(batch, seq, d)
```

```python
import functools
import math

import jax
import jax.numpy as jnp
from jax import lax
from jax.experimental import pallas as pl
from jax.experimental.pallas import tpu as pltpu

F32 = jnp.float32
BF16 = jnp.bfloat16

EPS = 1e-6
NEG_INF = -1e30
LOG2E = math.log2(math.e)
LANES = 128
N_HEADS = 16
HEAD_DIM = 128
SGU_CHUNK = 128
SGU_GROUPS = 16
VMEM_LIMIT_BYTES = 56 * 1024 * 1024


def _params(*sem, flags=None):
    return pltpu.CompilerParams(dimension_semantics=sem,
                                vmem_limit_bytes=VMEM_LIMIT_BYTES, flags=flags)


def _rmsnorm_rows(x, g):
    ms = jnp.mean(x * x, axis=-1, keepdims=True)
    return x * lax.rsqrt(ms + EPS) * g


def _gelu_tanh(x):
    c = math.sqrt(2.0 / math.pi)
    return 0.5 * x * (1.0 + jnp.tanh(c * (x + 0.044715 * (x * x * x))))


def _relu2(x):
    a = jnp.maximum(x, 0.0)
    return a * a


_ACTS = {"none": lambda x: x, "gelu": _gelu_tanh}


def _norm_matmul_body(x_ref, g_ref, w_ref, b_ref, gn_ref, o_ref, xn_ref, *,
                      act, norm_lo, norm_hi, rc):
    j = pl.program_id(1)
    bm, bn = o_ref.shape
    in_norm = jnp.logical_and(j >= norm_lo, j < norm_hi)

    def step(first):
        for r in range(bm // rc):
            rows = slice(r * rc, (r + 1) * rc)
            if first:
                xn = _rmsnorm_rows(x_ref[rows, :], g_ref[...]).astype(BF16)
                xn_ref[rows, :] = xn
            else:
                xn = xn_ref[rows, :]
            acc = jnp.dot(xn, w_ref[...], preferred_element_type=F32)
            a = _ACTS[act](acc + b_ref[...])
            if norm_hi <= norm_lo:
                o_ref[rows, :] = a.astype(o_ref.dtype)
                continue
            for c in range(bn // LANES):
                sl = slice(c * LANES, (c + 1) * LANES)
                blk = a[:, sl]
                ms = jnp.mean(blk * blk, axis=-1, keepdims=True)
                scale = jnp.where(in_norm, lax.rsqrt(ms + EPS), 1.0)
                o_ref[rows, sl] = (blk * scale * gn_ref[:, sl]).astype(o_ref.dtype)

    pl.when(j == 0)(lambda: step(True))
    pl.when(j > 0)(lambda: step(False))


def _norm_matmul(x, g, w, bias, gain, *, act, norm_cols, bm, bn, rc=256):
    m, k = x.shape
    n = w.shape[1]
    lo, hi = norm_cols
    body = functools.partial(_norm_matmul_body, act=act,
                             norm_lo=lo // bn, norm_hi=hi // bn, rc=rc)
    return pl.pallas_call(
        body,
        grid=(m // bm, n // bn),
        in_specs=[
            pl.BlockSpec((bm, k), lambda i, j: (i, 0)),
            pl.BlockSpec((1, k), lambda i, j: (0, 0)),
            pl.BlockSpec((k, bn), lambda i, j: (0, j)),
            pl.BlockSpec((1, bn), lambda i, j: (0, j)),
            pl.BlockSpec((1, bn), lambda i, j: (0, j)),
        ],
        out_specs=pl.BlockSpec((bm, bn), lambda i, j: (i, j)),
        out_shape=jax.ShapeDtypeStruct((m, n), BF16),
        scratch_shapes=[pltpu.VMEM((bm, k), BF16)],
        compiler_params=_params("parallel", "arbitrary"),
        name="norm_matmul_" + act,
    )(x, g, w, bias, gain)


def _matmul_res_body(a_ref, w_ref, r_ref, o_ref):
    o_ref[...] = r_ref[...] + jnp.dot(a_ref[...], w_ref[...],
                                      preferred_element_type=F32)


def _matmul_res(a, w, res, *, bm, bn):
    m, k = a.shape
    n = w.shape[1]
    return pl.pallas_call(
        _matmul_res_body,
        grid=(m // bm, n // bn),
        in_specs=[
            pl.BlockSpec((bm, k), lambda i, j: (i, 0)),
            pl.BlockSpec((k, bn), lambda i, j: (0, j)),
            pl.BlockSpec((bm, bn), lambda i, j: (i, j)),
        ],
        out_specs=pl.BlockSpec((bm, bn), lambda i, j: (i, j)),
        out_shape=jax.ShapeDtypeStruct((m, n), F32),
        compiler_params=_params("parallel", "arbitrary"),
        name="matmul_res_k%d" % k,
    )(a, w, res)


def _split3(x):
    hi = x.astype(BF16).astype(F32)
    r = x - hi
    mid = r.astype(BF16).astype(F32)
    lo = (r - mid).astype(BF16).astype(F32)
    return hi, mid, lo


def _tail_selectors():
    r = jnp.arange(LANES)[:, None]
    col = jnp.arange(N_HEADS * LANES)[None, :]
    h, lane = col // LANES, col % LANES
    part = (r - h) // N_HEADS
    is_part = jnp.logical_and((r - h) % N_HEADS == 0, jnp.logical_and(part >= 0, part < 3))
    one_row = r == 3 * N_HEADS
    sq = jnp.where(jnp.logical_and(is_part, lane == part), 1.0, 0.0)
    sq = jnp.where(jnp.logical_and(one_row, jnp.logical_and(lane >= 3, lane < 6)), 1.0, sq)
    sk = jnp.where(jnp.logical_and(is_part, lane == part + 3), -1.0, 0.0)
    sk = jnp.where(jnp.logical_and(one_row, lane < 3), 1.0, sk)
    return jnp.concatenate([sq, sk], axis=1).astype(BF16)


def _fox_gate_body(x_ref, g_ref, wf_ref, bf_ref, sel_ref, qt_ref, kt_ref, carry_ref):
    t = pl.program_id(1)
    bm = x_ref.shape[0]

    @pl.when(t == 0)
    def _():
        carry_ref[...] = jnp.zeros_like(carry_ref)

    xn = _rmsnorm_rows(x_ref[...], g_ref[...]).astype(BF16)
    fl = jnp.dot(xn, wf_ref[...], preferred_element_type=F32) + bf_ref[...]
    logf = jnp.minimum(fl, 0.0) - jnp.log1p(jnp.exp(-jnp.abs(fl)))

    row = lax.broadcasted_iota(jnp.int32, (bm, bm), 0)
    col = lax.broadcasted_iota(jnp.int32, (bm, bm), 1)
    tri = jnp.where(col <= row, 1.0, 0.0).astype(BF16)
    hi, mid, lo = _split3(logf)
    parts = jnp.concatenate([hi, mid, lo], axis=1).astype(BF16)
    ps = jnp.dot(tri, parts, preferred_element_type=F32)
    c = (ps[:, :LANES] + ps[:, LANES:2 * LANES] + ps[:, 2 * LANES:]) + carry_ref[...]
    carry_ref[...] = c[bm - 1:bm, :]

    ch, cm, cl = _split3(c * LOG2E)
    lane = lax.broadcasted_iota(jnp.int32, (bm, LANES), 1)
    packed = jnp.where(lane < N_HEADS, ch, jnp.where(
        lane < 2 * N_HEADS, pltpu.roll(cm, N_HEADS, 1), jnp.where(
            lane < 3 * N_HEADS, pltpu.roll(cl, 2 * N_HEADS, 1), jnp.where(
                lane == 3 * N_HEADS, 1.0, 0.0))))
    tails = jnp.dot(packed.astype(BF16), sel_ref[...],
                    preferred_element_type=F32).astype(BF16)
    for h in range(N_HEADS):
        qt_ref[0, h] = tails[:, h * LANES:(h + 1) * LANES]
        kt_ref[0, h] = tails[:, (N_HEADS + h) * LANES:(N_HEADS + h + 1) * LANES]


def _fox_gate(x, g, wf, bf, *, batch, seq, bm):
    m, k = x.shape
    nt = seq // bm
    tail = jax.ShapeDtypeStruct((batch, N_HEADS, seq, LANES), BF16)
    return pl.pallas_call(
        _fox_gate_body,
        grid=(batch, nt),
        in_specs=[
            pl.BlockSpec((bm, k), lambda b, t: (b * nt + t, 0)),
            pl.BlockSpec((1, k), lambda b, t: (0, 0)),
            pl.BlockSpec((k, LANES), lambda b, t: (0, 0)),
            pl.BlockSpec((1, LANES), lambda b, t: (0, 0)),
            pl.BlockSpec((LANES, 2 * N_HEADS * LANES), lambda b, t: (0, 0)),
        ],
        out_specs=[
            pl.BlockSpec((1, N_HEADS, bm, LANES), lambda b, t: (b, 0, t, 0)),
            pl.BlockSpec((1, N_HEADS, bm, LANES), lambda b, t: (b, 0, t, 0)),
        ],
        out_shape=[tail, tail],
        scratch_shapes=[pltpu.VMEM((1, LANES), F32)],
        compiler_params=_params("parallel", "arbitrary"),
        name="fox_gate",
    )(x, g, wf, bf, _tail_selectors())


V_ROWS = HEAD_DIM + 16


def _fox_attn_body(q_ref, qt_ref, k_ref, kt_ref, v_ref, o_ref,
                   vT_ref, mask_ref, qT_ref, s0_ref, s1_ref, p0_ref, p1_ref, acc_ref,
                   alpha_ref, *, tq, tk):
    qi = pl.program_id(2)
    n_kv = v_ref.shape[0] // tk
    s_refs = (s0_ref, s1_ref)
    p_refs = (p0_ref, p1_ref)

    @pl.when(qi == 0)
    def _():
        ones_row = jnp.where(
            lax.broadcasted_iota(jnp.int32, (V_ROWS - HEAD_DIM, tk), 0) == 0, 1.0, 0.0)
        for c in range(n_kv):
            blk = v_ref[c * tk:(c + 1) * tk, :].astype(F32)
            vT_ref[c, 0:HEAD_DIM, :] = blk.T.astype(BF16)
            vT_ref[c, HEAD_DIM:V_ROWS, :] = ones_row.astype(BF16)
        r = lax.broadcasted_iota(jnp.int32, (tk, tq), 0)
        c = lax.broadcasted_iota(jnp.int32, (tk, tq), 1)
        mask_ref[0] = jnp.zeros((tk, tq), F32)
        mask_ref[1] = jnp.where(r <= c, 0.0, NEG_INF)
        mask_ref[2] = jnp.where(r + tk <= c, 0.0, NEG_INF)

    qT_ref[0:HEAD_DIM, :] = q_ref[...].astype(F32).T.astype(BF16)
    qT_ref[HEAD_DIM:2 * HEAD_DIM, :] = qt_ref[0, 0].astype(F32).T.astype(BF16)
    acc_ref[...] = jnp.zeros_like(acc_ref)

    def stage_a(t):
        cmax = None
        for half, s_ref in enumerate(s_refs):
            j = 2 * t + half
            off = pl.multiple_of(j * tk, tk)
            k = jnp.concatenate([k_ref[pl.ds(off, tk), :],
                                 kt_ref[0, 0, pl.ds(off, tk), :]], axis=1)
            s = jnp.dot(k, qT_ref[...], preferred_element_type=F32)
            s = s + mask_ref[jnp.clip(j - 2 * qi + 1, 0, 2)]
            s_ref[...] = s
            cm = jnp.max(s, axis=0, keepdims=True)
            cmax = cm if cmax is None else jnp.maximum(cmax, cm)
        return cmax

    def stage_c(t, alpha):
        pv = (jnp.dot(vT_ref[2 * t], p0_ref[...], preferred_element_type=F32) +
              jnp.dot(vT_ref[2 * t + 1], p1_ref[...], preferred_element_type=F32))
        acc_ref[...] = alpha * acc_ref[...] + pv

    def trip(t, carry, do_c, do_a):
        m_prev, alpha_prev, cmax = carry
        m_new = jnp.maximum(m_prev, cmax)
        alpha = jnp.exp2(m_prev - m_new)
        ps = [jnp.exp2(s_ref[...] - m_new).astype(BF16) for s_ref in s_refs]
        if do_c:
            stage_c(t - 1, alpha_prev)
        for p_ref, p in zip(p_refs, ps):
            p_ref[...] = p
        return m_new, alpha, (stage_a(t + 1) if do_a else cmax)

    init = (jnp.full((1, tq), NEG_INF, F32), jnp.ones((1, tq), F32), stage_a(0))

    @pl.when(qi == 0)
    def _():
        _, alpha, _ = trip(0, init, False, False)
        alpha_ref[0:1, :] = alpha

    @pl.when(qi > 0)
    def _():
        carry = trip(0, init, False, True)
        carry = lax.fori_loop(1, qi, lambda t, c: trip(t, c, True, True), carry)
        _, alpha, _ = trip(qi, carry, True, False)
        alpha_ref[0:1, :] = alpha

    stage_c(qi, alpha_ref[0:1, :])
    acc = acc_ref[...]
    o = acc[0:HEAD_DIM, :] / acc[HEAD_DIM:HEAD_DIM + 1, :]
    o_ref[...] = o.T.astype(o_ref.dtype)


def _fox_attention(qkv, qt, kt, *, batch, seq, tq):
    m = qkv.shape[0]
    nq = seq // tq
    tk = tq // 2
    d = N_HEADS * HEAD_DIM
    return pl.pallas_call(
        functools.partial(_fox_attn_body, tq=tq, tk=tk),
        grid=(batch, N_HEADS, nq),
        in_specs=[
            pl.BlockSpec((tq, HEAD_DIM), lambda b, h, i: (b * nq + i, h)),
            pl.BlockSpec((1, 1, tq, LANES), lambda b, h, i: (b, h, i, 0)),
            pl.BlockSpec((seq, HEAD_DIM), lambda b, h, i: (b, N_HEADS + h)),
            pl.BlockSpec((1, 1, seq, LANES), lambda b, h, i: (b, h, 0, 0)),
            pl.BlockSpec((seq, HEAD_DIM), lambda b, h, i: (b, 2 * N_HEADS + h)),
        ],
        out_specs=pl.BlockSpec((tq, HEAD_DIM), lambda b, h, i: (b * nq + i, h)),
        out_shape=jax.ShapeDtypeStruct((m, d), BF16),
        scratch_shapes=[
            pltpu.VMEM((seq // tk, V_ROWS, tk), BF16),
            pltpu.VMEM((3, tk, tq), F32),
            pltpu.VMEM((2 * HEAD_DIM, tq), BF16),
            pltpu.VMEM((tk, tq), F32), pltpu.VMEM((tk, tq), F32),
            pltpu.VMEM((tk, tq), BF16), pltpu.VMEM((tk, tq), BF16),
            pltpu.VMEM((V_ROWS, tq), F32),
            pltpu.VMEM((8, tq), F32),
        ],
        compiler_params=_params("parallel", "parallel", "arbitrary"),
        name="fox_attention",
    )(qkv, qt, qkv, kt, qkv)


def _sgu_out_body(u_ref, v_ref, ws_ref, bs_ref, w_ref, r_ref, o_ref, gated_ref):
    j = pl.program_id(1)
    bm = u_ref.shape[0]

    @pl.when(j == 0)
    def _():
        row = lax.broadcasted_iota(jnp.int32, (SGU_CHUNK, SGU_CHUNK), 0)
        col = lax.broadcasted_iota(jnp.int32, (SGU_CHUNK, SGU_CHUNK), 1)
        causal = col <= row
        for g in range(SGU_GROUPS):
            cs = slice(g * LANES, (g + 1) * LANES)
            wc = jnp.where(causal, ws_ref[g], jnp.zeros_like(ws_ref[g]))
            for c in range(bm // SGU_CHUNK):
                rs = slice(c * SGU_CHUNK, (c + 1) * SGU_CHUNK)
                mixed = jnp.dot(wc, v_ref[rs, cs],
                                preferred_element_type=F32) + bs_ref[g]
                gated_ref[rs, cs] = (u_ref[rs, cs].astype(F32) * mixed).astype(BF16)

    o_ref[...] = r_ref[...] + jnp.dot(gated_ref[...], w_ref[...],
                                      preferred_element_type=F32)


def _sgu_out(uv, ws, bsb, w, res, *, bm, bn):
    m = uv.shape[0]
    k, n = w.shape
    return pl.pallas_call(
        _sgu_out_body,
        grid=(m // bm, n // bn),
        in_specs=[
            pl.BlockSpec((bm, k), lambda i, j: (i, 0)),
            pl.BlockSpec((bm, k), lambda i, j: (i, 1)),
            pl.BlockSpec(ws.shape, lambda i, j: (0, 0, 0)),
            pl.BlockSpec(bsb.shape, lambda i, j: (0, 0, 0)),
            pl.BlockSpec((k, bn), lambda i, j: (0, j)),
            pl.BlockSpec((bm, bn), lambda i, j: (i, j)),
        ],
        out_specs=pl.BlockSpec((bm, bn), lambda i, j: (i, j)),
        out_shape=jax.ShapeDtypeStruct((m, n), F32),
        scratch_shapes=[pltpu.VMEM((bm, k), BF16)],
        compiler_params=_params("parallel", "arbitrary"),
        name="sgu_out",
    )(uv, uv, ws, bsb, w, res)


def _row(v):
    return v.reshape(1, -1).astype(F32)


MLP_ROWS = 512


def _mlp_body(x_ref, g_ref, w1_ref, w2_ref, o_ref, xn_ref):
    c = pl.program_id(1)
    bm = x_ref.shape[0]

    def step(first):
        for r in range(bm // MLP_ROWS):
            rows = slice(r * MLP_ROWS, (r + 1) * MLP_ROWS)
            if first:
                x = x_ref[rows, :]
                xn = _rmsnorm_rows(x, g_ref[...]).astype(BF16)
                xn_ref[rows, :] = xn
            else:
                x = o_ref[rows, :]
                xn = xn_ref[rows, :]
            h = jnp.dot(xn, w1_ref[...], preferred_element_type=F32)
            o_ref[rows, :] = x + jnp.dot(_relu2(h).astype(BF16), w2_ref[...],
                                         preferred_element_type=F32)

    pl.when(c == 0)(lambda: step(True))
    pl.when(c > 0)(lambda: step(False))


def _mlp(x, g, w1, w2, *, bm=1024, bf=512):
    m, d = x.shape
    f = w1.shape[1]
    return pl.pallas_call(
        _mlp_body,
        grid=(m // bm, f // bf),
        in_specs=[
            pl.BlockSpec((bm, d), lambda i, c: (i, 0)),
            pl.BlockSpec((1, d), lambda i, c: (0, 0)),
            pl.BlockSpec((d, bf), lambda i, c: (0, c)),
            pl.BlockSpec((bf, d), lambda i, c: (c, 0)),
        ],
        out_specs=pl.BlockSpec((bm, d), lambda i, c: (i, 0)),
        out_shape=jax.ShapeDtypeStruct((m, d), F32),
        scratch_shapes=[pltpu.VMEM((bm, d), BF16)],
        compiler_params=_params("parallel", "arbitrary"),
        name="mlp_relu2",
    )(x, _row(g), w1.astype(BF16), w2.astype(BF16))


def _fox_layer(x, g, w_in, b_f, g_q, g_k, w_out, *, batch, seq):
    d = N_HEADS * HEAD_DIM
    w_qkv = w_in[:, :3 * d].astype(BF16)
    w_f = jnp.pad(w_in[:, 3 * d:], ((0, 0), (0, LANES - N_HEADS))).astype(BF16)
    b_fp = jnp.pad(b_f, (0, LANES - N_HEADS)).reshape(1, LANES).astype(F32)
    gain = jnp.concatenate([jnp.tile(g_q * (HEAD_DIM ** -0.5 * LOG2E), N_HEADS),
                            jnp.tile(g_k, N_HEADS), jnp.ones((d,), F32)])
    qkv = _norm_matmul(x, _row(g), w_qkv, jnp.zeros((1, 3 * d), F32), _row(gain),
                       act="none", norm_cols=(0, 2 * d), bm=1024, bn=1024)
    qt, kt = _fox_gate(x, _row(g), w_f, b_fp, batch=batch, seq=seq, bm=512)
    o = _fox_attention(qkv, qt, kt, batch=batch, seq=seq, tq=1024)
    return _matmul_res(o, w_out.astype(BF16), x, bm=2048, bn=512)


def _sgu_layer(x, g, w_in, b_in, g_v, w_s, b_s, w_out):
    width = w_out.shape[0]
    gain = jnp.concatenate([jnp.ones((width,), F32), g_v.reshape(-1)])
    uv = _norm_matmul(x, _row(g), w_in.astype(BF16), _row(b_in), _row(gain),
                      act="gelu", norm_cols=(width, 2 * width), bm=1024, bn=1024)
    bsb = jnp.broadcast_to(b_s[:, :, None], b_s.shape + (LANES,)).astype(F32)
    return _sgu_out(uv, w_s.astype(BF16), bsb, w_out.astype(BF16), x, bm=1024, bn=1024)


def kernel(x, g_mix, g_mlp, fox_w_in, fox_b_f, fox_g_q, fox_g_k, fox_w_out,
           sgu_w_in, sgu_b_in, sgu_g_v, sgu_w_s, sgu_b_s, sgu_w_out, mlp_w1, mlp_w2):
    batch, seq, d = x.shape
    depth = g_mix.shape[0]
    x = x.reshape(batch * seq, d)
    for i in range(depth):
        j = i // 2
        if i % 2 == 0:
            x = _fox_layer(x, g_mix[i], fox_w_in[j], fox_b_f[j], fox_g_q[j],
                           fox_g_k[j], fox_w_out[j], batch=batch, seq=seq)
        else:
            x = _sgu_layer(x, g_mix[i], sgu_w_in[j], sgu_b_in[j], sgu_g_v[j],
                           sgu_w_s[j], sgu_b_s[j], sgu_w_out[j])
        x = _mlp(x, g_mlp[i], mlp_w1[i], mlp_w2[i])
    return x.reshape(batch, seq, d)
```

```python
import functools
import math

import jax
import jax.numpy as jnp
from jax import lax
from jax.experimental import pallas as pl
from jax.experimental.pallas import tpu as pltpu

F32 = jnp.float32
BF16 = jnp.bfloat16

EPS = 1e-6
NEG_INF = -1e30
LOG2E = math.log2(math.e)
LANES = 128
N_HEADS = 16
HEAD_DIM = 128
SGU_CHUNK = 128
SGU_GROUPS = 16
VMEM_LIMIT_BYTES = 56 * 1024 * 1024


def _params(*sem, flags=None):
    return pltpu.CompilerParams(dimension_semantics=sem,
                                vmem_limit_bytes=VMEM_LIMIT_BYTES, flags=flags)


def _rmsnorm_rows(x, g):
    ms = jnp.mean(x * x, axis=-1, keepdims=True)
    return x * lax.rsqrt(ms + EPS) * g


def _gelu_tanh(x):
    k0 = -2.0 * math.sqrt(2.0 / math.pi) * LOG2E
    z = x * (k0 + (k0 * 0.044715) * (x * x))
    return x / (1.0 + jnp.exp2(z))


def _relu2(x):
    a = jnp.maximum(x, 0.0)
    return a * a


_ACTS = {"none": lambda x: x, "gelu": _gelu_tanh}


def _norm_matmul_body(x_ref, g_ref, w_ref, b_ref, gn_ref, o_ref, xn_ref, *,
                      act, norm_lo, norm_hi, rc, head_major):
    j = pl.program_id(1)
    bm, bn = x_ref.shape[0], w_ref.shape[1]
    in_norm = jnp.logical_and(j >= norm_lo, j < norm_hi)

    def step(first):
        for r in range(bm // rc):
            rows = slice(r * rc, (r + 1) * rc)
            if first:
                xn = _rmsnorm_rows(x_ref[rows, :], g_ref[...]).astype(BF16)
                xn_ref[rows, :] = xn
            else:
                xn = xn_ref[rows, :]
            acc = jnp.dot(xn, w_ref[...], preferred_element_type=F32)
            a = _ACTS[act](acc + b_ref[...])
            if norm_hi <= norm_lo:
                o_ref[rows, :] = a.astype(o_ref.dtype)
                continue
            for c in range(bn // LANES):
                sl = slice(c * LANES, (c + 1) * LANES)
                blk = a[:, sl]
                ms = jnp.mean(blk * blk, axis=-1, keepdims=True)
                scale = jnp.where(in_norm, lax.rsqrt(ms + EPS), 1.0)
                y = (blk * scale * gn_ref[:, sl]).astype(o_ref.dtype)
                if head_major:
                    o_ref[c, rows, :] = y
                else:
                    o_ref[rows, sl] = y

    pl.when(j == 0)(lambda: step(True))
    pl.when(j > 0)(lambda: step(False))


def _norm_matmul(x, g, w, bias, gain, *, act, norm_cols, n, bm, bn, rc=256,
                 head_major=False):
    m, k = x.shape
    lo, hi = norm_cols
    body = functools.partial(_norm_matmul_body, act=act, norm_lo=lo // bn,
                             norm_hi=hi // bn, rc=rc, head_major=head_major)
    if head_major:
        out_spec = pl.BlockSpec((bn // LANES, bm, LANES), lambda i, j: (j, i, 0))
        out_shape = jax.ShapeDtypeStruct((n // LANES, m, LANES), BF16)
    else:
        out_spec = pl.BlockSpec((bm, bn), lambda i, j: (i, j))
        out_shape = jax.ShapeDtypeStruct((m, n), BF16)
    return pl.pallas_call(
        body,
        grid=(m // bm, n // bn),
        in_specs=[
            pl.BlockSpec((bm, k), lambda i, j: (i, 0)),
            pl.BlockSpec((1, k), lambda i, j: (0, 0)),
            pl.BlockSpec((k, bn), lambda i, j: (0, j)),
            pl.BlockSpec((1, bn), lambda i, j: (0, j)),
            pl.BlockSpec((1, bn), lambda i, j: (0, j)),
        ],
        out_specs=out_spec,
        out_shape=out_shape,
        scratch_shapes=[pltpu.VMEM((bm, k), BF16)],
        compiler_params=_params("parallel", "arbitrary"),
        name="norm_matmul_" + act,
    )(x, g, w, bias, gain)


def _matmul_res_body(a_ref, w_ref, r_ref, o_ref):
    o_ref[...] = r_ref[...] + jnp.dot(a_ref[...], w_ref[...],
                                      preferred_element_type=F32)


def _matmul_res(a, w, res, *, bm):
    m, k = a.shape
    n = w.shape[1]
    return pl.pallas_call(
        _matmul_res_body,
        grid=(m // bm,),
        in_specs=[
            pl.BlockSpec((bm, k), lambda i: (i, 0)),
            pl.BlockSpec((k, n), lambda i: (0, 0)),
            pl.BlockSpec((bm, n), lambda i: (i, 0)),
        ],
        out_specs=pl.BlockSpec((bm, n), lambda i: (i, 0)),
        out_shape=jax.ShapeDtypeStruct((m, n), F32),
        compiler_params=_params("parallel"),
        name="matmul_res_k%d" % k,
    )(a, w, res)


def _split3(x):
    hi = x.astype(BF16).astype(F32)
    r = x - hi
    mid = r.astype(BF16).astype(F32)
    lo = (r - mid).astype(BF16).astype(F32)
    return hi, mid, lo


def _tail_selectors():
    r = jnp.arange(LANES)[:, None]
    col = jnp.arange(N_HEADS * LANES)[None, :]
    h, lane = col // LANES, col % LANES
    part = (r - h) // N_HEADS
    is_part = jnp.logical_and((r - h) % N_HEADS == 0, jnp.logical_and(part >= 0, part < 3))
    one_row = r == 3 * N_HEADS
    sq = jnp.where(jnp.logical_and(is_part, lane == part), 1.0, 0.0)
    sq = jnp.where(jnp.logical_and(one_row, jnp.logical_and(lane >= 3, lane < 6)), 1.0, sq)
    sk = jnp.where(jnp.logical_and(is_part, lane == part + 3), -1.0, 0.0)
    sk = jnp.where(jnp.logical_and(one_row, lane < 3), 1.0, sk)
    return jnp.concatenate([sq, sk], axis=1).astype(BF16)


def _fox_gate_body(x_ref, g_ref, wf_ref, bf_ref, sel_ref, qt_ref, kt_ref, carry_ref):
    t = pl.program_id(1)
    bm = x_ref.shape[0]

    @pl.when(t == 0)
    def _():
        carry_ref[...] = jnp.zeros_like(carry_ref)

    xn = _rmsnorm_rows(x_ref[...], g_ref[...]).astype(BF16)
    fl = jnp.dot(xn, wf_ref[...], preferred_element_type=F32) + bf_ref[...]
    logf = jnp.minimum(fl, 0.0) - jnp.log1p(jnp.exp(-jnp.abs(fl)))

    row = lax.broadcasted_iota(jnp.int32, (bm, bm), 0)
    col = lax.broadcasted_iota(jnp.int32, (bm, bm), 1)
    tri = jnp.where(col <= row, 1.0, 0.0).astype(BF16)
    hi, mid, lo = _split3(logf)
    parts = jnp.concatenate([hi, mid, lo], axis=1).astype(BF16)
    ps = jnp.dot(tri, parts, preferred_element_type=F32)
    c = (ps[:, :LANES] + ps[:, LANES:2 * LANES] + ps[:, 2 * LANES:]) + carry_ref[...]
    carry_ref[...] = c[bm - 1:bm, :]

    ch, cm, cl = _split3(c * LOG2E)
    lane = lax.broadcasted_iota(jnp.int32, (bm, LANES), 1)
    packed = jnp.where(lane < N_HEADS, ch, jnp.where(
        lane < 2 * N_HEADS, pltpu.roll(cm, N_HEADS, 1), jnp.where(
            lane < 3 * N_HEADS, pltpu.roll(cl, 2 * N_HEADS, 1), jnp.where(
                lane == 3 * N_HEADS, 1.0, 0.0))))
    tails = jnp.dot(packed.astype(BF16), sel_ref[...],
                    preferred_element_type=F32).astype(BF16)
    for h in range(N_HEADS):
        qt_ref[0, h] = tails[:, h * LANES:(h + 1) * LANES]
        kt_ref[0, h] = tails[:, (N_HEADS + h) * LANES:(N_HEADS + h + 1) * LANES]


def _fox_gate(x, g, wf, bf, *, batch, seq, bm):
    m, k = x.shape
    nt = seq // bm
    tail = jax.ShapeDtypeStruct((batch, N_HEADS, seq, LANES), BF16)
    return pl.pallas_call(
        _fox_gate_body,
        grid=(batch, nt),
        in_specs=[
            pl.BlockSpec((bm, k), lambda b, t: (b * nt + t, 0)),
            pl.BlockSpec((1, k), lambda b, t: (0, 0)),
            pl.BlockSpec((k, LANES), lambda b, t: (0, 0)),
            pl.BlockSpec((1, LANES), lambda b, t: (0, 0)),
            pl.BlockSpec((LANES, 2 * N_HEADS * LANES), lambda b, t: (0, 0)),
        ],
        out_specs=[
            pl.BlockSpec((1, N_HEADS, bm, LANES), lambda b, t: (b, 0, t, 0)),
            pl.BlockSpec((1, N_HEADS, bm, LANES), lambda b, t: (b, 0, t, 0)),
        ],
        out_shape=[tail, tail],
        scratch_shapes=[pltpu.VMEM((1, LANES), F32)],
        compiler_params=_params("parallel", "arbitrary"),
        name="fox_gate",
    )(x, g, wf, bf, _tail_selectors())


V_ROWS = HEAD_DIM + 16


def _fox_attn_body(q_ref, qt_ref, k_ref, kt_ref, v_ref, o_ref,
                   vT_ref, mask_ref, qT_ref, s0_ref, s1_ref, p0_ref, p1_ref, acc_ref,
                   alpha_ref, *, tq, tk):
    qi = pl.program_id(2)
    n_kv = v_ref.shape[0] // tk
    s_refs = (s0_ref, s1_ref)
    p_refs = (p0_ref, p1_ref)

    @pl.when(qi == 0)
    def _():
        ones_row = jnp.where(
            lax.broadcasted_iota(jnp.int32, (V_ROWS - HEAD_DIM, tk), 0) == 0, 1.0, 0.0)
        for c in range(n_kv):
            blk = v_ref[c * tk:(c + 1) * tk, :].astype(F32)
            vT_ref[c, 0:HEAD_DIM, :] = blk.T.astype(BF16)
            vT_ref[c, HEAD_DIM:V_ROWS, :] = ones_row.astype(BF16)
        r = lax.broadcasted_iota(jnp.int32, (tk, tq), 0)
        c = lax.broadcasted_iota(jnp.int32, (tk, tq), 1)
        mask_ref[0] = jnp.zeros((tk, tq), F32)
        mask_ref[1] = jnp.where(r <= c, 0.0, NEG_INF)
        mask_ref[2] = jnp.where(r + tk <= c, 0.0, NEG_INF)

    qT_ref[0:HEAD_DIM, :] = q_ref[...].astype(F32).T.astype(BF16)
    qT_ref[HEAD_DIM:2 * HEAD_DIM, :] = qt_ref[0, 0].astype(F32).T.astype(BF16)
    acc_ref[...] = jnp.zeros_like(acc_ref)

    def stage_a(t):
        cmax = None
        for half, s_ref in enumerate(s_refs):
            j = 2 * t + half
            off = pl.multiple_of(j * tk, tk)
            k = jnp.concatenate([k_ref[pl.ds(off, tk), :],
                                 kt_ref[0, 0, pl.ds(off, tk), :]], axis=1)
            s = jnp.dot(k, qT_ref[...], preferred_element_type=F32)
            s = s + mask_ref[jnp.clip(j - 2 * qi + 1, 0, 2)]
            s_ref[...] = s
            cm = jnp.max(s, axis=0, keepdims=True)
            cmax = cm if cmax is None else jnp.maximum(cmax, cm)
        return cmax

    def stage_c(t, alpha):
        pv = (jnp.dot(vT_ref[2 * t], p0_ref[...], preferred_element_type=F32) +
              jnp.dot(vT_ref[2 * t + 1], p1_ref[...], preferred_element_type=F32))
        acc_ref[...] = alpha * acc_ref[...] + pv

    def trip(t, carry, do_c, do_a):
        m_prev, alpha_prev, cmax = carry
        m_new = jnp.maximum(m_prev, cmax)
        alpha = jnp.exp2(m_prev - m_new)
        ps = [jnp.exp2(s_ref[...] - m_new).astype(BF16) for s_ref in s_refs]
        if do_c:
            stage_c(t - 1, alpha_prev)
        for p_ref, p in zip(p_refs, ps):
            p_ref[...] = p
        return m_new, alpha, (stage_a(t + 1) if do_a else cmax)

    init = (jnp.full((1, tq), NEG_INF, F32), jnp.ones((1, tq), F32), stage_a(0))

    @pl.when(qi == 0)
    def _():
        _, alpha, _ = trip(0, init, False, False)
        alpha_ref[0:1, :] = alpha

    @pl.when(qi > 0)
    def _():
        carry = trip(0, init, False, True)
        carry = lax.fori_loop(1, qi, lambda t, c: trip(t, c, True, True), carry)
        _, alpha, _ = trip(qi, carry, True, False)
        alpha_ref[0:1, :] = alpha

    stage_c(qi, alpha_ref[0:1, :])
    acc = acc_ref[...]
    o = acc[0:HEAD_DIM, :] / acc[HEAD_DIM:HEAD_DIM + 1, :]
    o_ref[...] = o.T.astype(o_ref.dtype)


def _fox_attention(qkv, qt, kt, *, batch, seq, tq):
    m = qkv.shape[1]
    nq = seq // tq
    tk = tq // 2
    d = N_HEADS * HEAD_DIM
    return pl.pallas_call(
        functools.partial(_fox_attn_body, tq=tq, tk=tk),
        grid=(batch, N_HEADS, nq),
        in_specs=[
            pl.BlockSpec((None, tq, HEAD_DIM), lambda b, h, i: (h, b * nq + i, 0)),
            pl.BlockSpec((1, 1, tq, LANES), lambda b, h, i: (b, h, i, 0)),
            pl.BlockSpec((None, seq, HEAD_DIM), lambda b, h, i: (N_HEADS + h, b, 0)),
            pl.BlockSpec((1, 1, seq, LANES), lambda b, h, i: (b, h, 0, 0)),
            pl.BlockSpec((None, seq, HEAD_DIM), lambda b, h, i: (2 * N_HEADS + h, b, 0)),
        ],
        out_specs=pl.BlockSpec((tq, HEAD_DIM), lambda b, h, i: (b * nq + i, h)),
        out_shape=jax.ShapeDtypeStruct((m, d), BF16),
        scratch_shapes=[
            pltpu.VMEM((seq // tk, V_ROWS, tk), BF16),
            pltpu.VMEM((3, tk, tq), F32),
            pltpu.VMEM((2 * HEAD_DIM, tq), BF16),
            pltpu.VMEM((tk, tq), F32), pltpu.VMEM((tk, tq), F32),
            pltpu.VMEM((tk, tq), BF16), pltpu.VMEM((tk, tq), BF16),
            pltpu.VMEM((V_ROWS, tq), F32),
            pltpu.VMEM((8, tq), F32),
        ],
        compiler_params=_params("parallel", "parallel", "arbitrary"),
        name="fox_attention",
    )(qkv, qt, qkv, kt, qkv)


SGU_ROWS = 2 * SGU_CHUNK


def _sgu_out_body(u_ref, v_ref, ws_ref, bs_ref, w_ref, r_ref, o_ref, gated_ref):
    bm = u_ref.shape[0]
    row = lax.broadcasted_iota(jnp.int32, (SGU_CHUNK, SGU_CHUNK), 0)
    col = lax.broadcasted_iota(jnp.int32, (SGU_CHUNK, SGU_CHUNK), 1)
    causal = col <= row
    wc = [jnp.where(causal, ws_ref[g], jnp.zeros_like(ws_ref[g]))
          for g in range(SGU_GROUPS)]
    for r in range(bm // SGU_ROWS):
        rows = slice(r * SGU_ROWS, (r + 1) * SGU_ROWS)
        for c in range(SGU_ROWS // SGU_CHUNK):
            rs = slice(r * SGU_ROWS + c * SGU_CHUNK, r * SGU_ROWS + (c + 1) * SGU_CHUNK)
            for g in range(SGU_GROUPS):
                cs = slice(g * LANES, (g + 1) * LANES)
                mixed = jnp.dot(wc[g], v_ref[rs, cs],
                                preferred_element_type=F32) + bs_ref[g]
                gated_ref[rs, cs] = (u_ref[rs, cs].astype(F32) * mixed).astype(BF16)
        o_ref[rows, :] = r_ref[rows, :] + jnp.dot(gated_ref[rows, :], w_ref[...],
                                                  preferred_element_type=F32)


def _sgu_out(uv, ws, bsb, w, res, *, bm):
    m = uv.shape[0]
    k, n = w.shape
    return pl.pallas_call(
        _sgu_out_body,
        grid=(m // bm,),
        in_specs=[
            pl.BlockSpec((bm, k), lambda i: (i, 0)),
            pl.BlockSpec((bm, k), lambda i: (i, 1)),
            pl.BlockSpec(ws.shape, lambda i: (0, 0, 0)),
            pl.BlockSpec(bsb.shape, lambda i: (0, 0, 0)),
            pl.BlockSpec((k, n), lambda i: (0, 0)),
            pl.BlockSpec((bm, n), lambda i: (i, 0)),
        ],
        out_specs=pl.BlockSpec((bm, n), lambda i: (i, 0)),
        out_shape=jax.ShapeDtypeStruct((m, n), F32),
        scratch_shapes=[pltpu.VMEM((bm, k), BF16)],
        compiler_params=_params("parallel"),
        name="sgu_out",
    )(uv, uv, ws, bsb, w, res)


def _row(v):
    return v.reshape(1, -1).astype(F32)


MLP_ROWS = 512


def _mlp_body(x_ref, g_ref, w1_ref, w2_ref, o_ref, xn_ref):
    c = pl.program_id(1)
    bm = x_ref.shape[0]

    def step(first):
        for r in range(bm // MLP_ROWS):
            rows = slice(r * MLP_ROWS, (r + 1) * MLP_ROWS)
            if first:
                x = x_ref[rows, :]
                xn = _rmsnorm_rows(x, g_ref[...]).astype(BF16)
                xn_ref[rows, :] = xn
            else:
                x = o_ref[rows, :]
                xn = xn_ref[rows, :]
            h = jnp.dot(xn, w1_ref[...], preferred_element_type=F32)
            o_ref[rows, :] = x + jnp.dot(_relu2(h).astype(BF16), w2_ref[...],
                                         preferred_element_type=F32)

    pl.when(c == 0)(lambda: step(True))
    pl.when(c > 0)(lambda: step(False))


def _mlp(x, g, w1, w2, *, bm=1024, bf=512):
    m, d = x.shape
    f = w1.shape[1]
    return pl.pallas_call(
        _mlp_body,
        grid=(m // bm, f // bf),
        in_specs=[
            pl.BlockSpec((bm, d), lambda i, c: (i, 0)),
            pl.BlockSpec((1, d), lambda i, c: (0, 0)),
            pl.BlockSpec((d, bf), lambda i, c: (0, c)),
            pl.BlockSpec((bf, d), lambda i, c: (c, 0)),
        ],
        out_specs=pl.BlockSpec((bm, d), lambda i, c: (i, 0)),
        out_shape=jax.ShapeDtypeStruct((m, d), F32),
        scratch_shapes=[pltpu.VMEM((bm, d), BF16)],
        compiler_params=_params("parallel", "arbitrary"),
        name="mlp_relu2",
    )(x, _row(g), w1.astype(BF16), w2.astype(BF16))


def _fox_layer(x, g, w_in, b_f, g_q, g_k, w_out, *, batch, seq):
    d = N_HEADS * HEAD_DIM
    w_qkv = w_in.astype(BF16)
    w_f = jnp.pad(w_in[:, 3 * d:], ((0, 0), (0, LANES - N_HEADS))).astype(BF16)
    b_fp = jnp.pad(b_f, (0, LANES - N_HEADS)).reshape(1, LANES).astype(F32)
    gain = jnp.concatenate([jnp.tile(g_q * (HEAD_DIM ** -0.5 * LOG2E), N_HEADS),
                            jnp.tile(g_k, N_HEADS), jnp.ones((d,), F32)])
    qkv = _norm_matmul(x, _row(g), w_qkv, jnp.zeros((1, 3 * d), F32), _row(gain),
                       act="none", norm_cols=(0, 2 * d), n=3 * d, bm=1024, bn=1024,
                       head_major=True)
    qt, kt = _fox_gate(x, _row(g), w_f, b_fp, batch=batch, seq=seq, bm=512)
    o = _fox_attention(qkv, qt, kt, batch=batch, seq=seq, tq=1024)
    return _matmul_res(o, w_out.astype(BF16), x, bm=512)


def _sgu_layer(x, g, w_in, b_in, g_v, w_s, b_s, w_out):
    width = w_out.shape[0]
    gain = jnp.concatenate([jnp.ones((width,), F32), g_v.reshape(-1)])
    uv = _norm_matmul(x, _row(g), w_in.astype(BF16), _row(b_in), _row(gain),
                      act="gelu", norm_cols=(width, 2 * width), n=2 * width,
                      bm=1024, bn=1024)
    bsb = jnp.broadcast_to(b_s[:, :, None], b_s.shape + (LANES,)).astype(F32)
    return _sgu_out(uv, w_s.astype(BF16), bsb, w_out.astype(BF16), x, bm=512)


def kernel(x, g_mix, g_mlp, fox_w_in, fox_b_f, fox_g_q, fox_g_k, fox_w_out,
           sgu_w_in, sgu_b_in, sgu_g_v, sgu_w_s, sgu_b_s, sgu_w_out, mlp_w1, mlp_w2):
    batch, seq, d = x.shape
    depth = g_mix.shape[0]
    x = x.reshape(batch * seq, d)
    for i in range(depth):
        j = i // 2
        if i % 2 == 0:
            x = _fox_layer(x, g_mix[i], fox_w_in[j], fox_b_f[j], fox_g_q[j],
                           fox_g_k[j], fox_w_out[j], batch=batch, seq=seq)
        else:
            x = _sgu_layer(x, g_mix[i], sgu_w_in[j], sgu_b_in[j], sgu_g_v[j],
                           sgu_w_s[j], sgu_b_s[j], sgu_w_out[j])
        x = _mlp(x, g_mlp[i], mlp_w1[i], mlp_w2[i])
    return x.reshape(batch, seq, d)
```
